```python
import math
import jax
import jax.numpy as jnp
from jax import lax
import numpy as np

D_MODEL = 1024
BATCH = 8
SEQ = 4096
DEPTH = 4

CTX_LEN = 256
GRID_W = 64
N_MIXERS = 2
N_A_LAYERS = (DEPTH + N_MIXERS - 1) // N_MIXERS
N_B_LAYERS = DEPTH // N_MIXERS

DN_HEADS = 8
DN_HEAD_DIM = D_MODEL // DN_HEADS
DN_INNER = DN_HEADS * DN_HEAD_DIM
DN_CONV = 5
DN_CHUNK = 64
DN_PROJ = 4 * DN_INNER + 4 * DN_HEADS

DA_HEADS = 8
DA_HEAD_DIM = D_MODEL // (2 * DA_HEADS)
DA_INNER = 2 * DA_HEADS * DA_HEAD_DIM
DA_QBLOCK = 128
ROPE_BASE = 10000.0

PEER_HEADS = 8
PEER_NKEYS = 128
PEER_NEXPERTS = PEER_NKEYS * PEER_NKEYS
PEER_DKEY = 256
PEER_TOPK = 16
PEER_TOKBLOCK = 128
PEER_V_SCALE = 0.5

ADA_SCALE = 0.5
EPS = 1e-6

kernel_name = 'hybrid_deltanet_diffattn_peer_dit'


def rms_norm(x, g):
    xf = x.astype(jnp.float32)
    y = xf * lax.rsqrt(jnp.mean(xf * xf, axis=-1, keepdims=True) + EPS)
    return (y * g.astype(jnp.float32)).astype(x.dtype)


def l2_norm(x):
    xf = x.astype(jnp.float32)
    return xf * lax.rsqrt(jnp.sum(xf * xf, axis=-1, keepdims=True) + EPS)


def modulate(h, shift, scale):
    return h * (1 + scale) + shift


def ada_modulation(cond, w_mod, b_mod):
    m = jax.nn.silu(cond) @ w_mod + b_mod
    return jnp.split(m[..., None, :], 6, axis=-1)


def depthwise_conv_centred(x, w):
    k = w.shape[0]
    return lax.conv_general_dilated(
        x, w[:, None, :].astype(x.dtype), window_strides=(1,),
        padding=[(k // 2, k // 2)], dimension_numbers=('NWC', 'WIO', 'NWC'),
        feature_group_count=x.shape[-1])


def gated_delta_chunked(q, k, v, beta, g, s0):
    f32 = jnp.float32
    b_, h_, L, dk = q.shape
    dv = v.shape[-1]
    c_ = DN_CHUNK
    n = L // c_
    q, k, v = (t.astype(f32).reshape(b_, h_, n, c_, -1) for t in (q, k, v))
    beta = beta.astype(f32).reshape(b_, h_, n, c_)
    G = jnp.cumsum(g.astype(f32).reshape(b_, h_, n, c_), axis=-1)
    idx = jnp.arange(c_)
    lower = idx[:, None] >= idx[None, :]
    gam = jnp.exp(jnp.where(lower, G[..., :, None] - G[..., None, :], -jnp.inf))
    kb = k * beta[..., None]
    a_mat = jnp.where(idx[:, None] > idx[None, :],
                      jnp.einsum('bhnid,bhnjd->bhnij', kb, k) * gam, 0.0)
    rhs = jnp.concatenate([v * beta[..., None], kb * jnp.exp(G)[..., None]], axis=-1)
    sol = lax.linalg.triangular_solve(a_mat + jnp.eye(c_, dtype=f32), rhs,
                                      left_side=True, lower=True, unit_diagonal=True)
    u, w = sol[..., :dv], sol[..., dv:]
    qk = jnp.einsum('bhnid,bhnjd->bhnij', q, k) * gam
    q_dec = q * jnp.exp(G)[..., None]
    k_dec = k * jnp.exp(G[..., -1:] - G)[..., None]
    last_decay = jnp.exp(G[..., -1])

    def step(s, inp):
        u_c, w_c, qk_c, qd_c, kd_c, ld_c = inp
        delta = u_c - w_c @ s
        o_c = qd_c @ s + qk_c @ delta
        s = s * ld_c[..., None, None] + jnp.einsum('bhcd,bhce->bhde', kd_c, delta)
        return s, o_c

    xs = tuple(jnp.moveaxis(t, 2, 0) for t in (u, w, qk, q_dec, k_dec, last_decay))
    s_final, o = lax.scan(step, s0.astype(f32), xs)
    return jnp.moveaxis(o, 0, 2).reshape(b_, h_, L, dv), s_final


def time_flip(t, direction):
    return t if direction == 0 else jnp.flip(t, axis=2)


def deltanet_mixer(a_ctx, a_lat, w_in, conv_w, a_log, dt_bias, norm_g, w_out, ctx_out):
    def project(h):
        b_, L, _ = h.shape
        p = h @ w_in
        qkv = jax.nn.silu(depthwise_conv_centred(p[..., :3 * DN_INNER], conv_w))
        qkv = jnp.moveaxis(qkv.reshape(b_, L, 3, DN_HEADS, DN_HEAD_DIM), 1, 3)
        q = l2_norm(qkv[:, 0]) * DN_HEAD_DIM ** -0.5
        k = l2_norm(qkv[:, 1])
        v = qkv[:, 2]
        z = p[..., 3 * DN_INNER:4 * DN_INNER].reshape(b_, L, DN_HEADS, DN_HEAD_DIM)
        ab = p[..., 4 * DN_INNER:].astype(jnp.float32).reshape(b_, L, 2, 2, DN_HEADS)
        beta = jax.nn.sigmoid(ab[:, :, 0])
        g = -jnp.exp(a_log.astype(jnp.float32)) * jax.nn.softplus(
            ab[:, :, 1] + dt_bias.astype(jnp.float32))
        return q, k, v, z, jnp.transpose(beta, (2, 0, 3, 1)), jnp.transpose(g, (2, 0, 3, 1))

    qc, kc, vc, zc, bc, gc = project(a_ctx)
    ql, kl, vl, zl, bl, gl = project(a_lat)
    outs_ctx, outs_lat = [], []
    for d in range(2):
        s0 = jnp.zeros(qc.shape[:2] + (DN_HEAD_DIM, DN_HEAD_DIM), jnp.float32)
        o_c, s_c = gated_delta_chunked(time_flip(qc, d), time_flip(kc, d), time_flip(vc, d),
                                       time_flip(bc[d], d), time_flip(gc[d], d), s0)
        o_l, _ = gated_delta_chunked(time_flip(ql, d), time_flip(kl, d), time_flip(vl, d),
                                     time_flip(bl[d], d), time_flip(gl[d], d), s_c)
        outs_ctx.append(time_flip(o_c, d))
        outs_lat.append(time_flip(o_l, d))

    def output(o, z):
        b_, L = z.shape[:2]
        o = jnp.swapaxes(o, 1, 2).astype(z.dtype)
        o = rms_norm(o, norm_g) * jax.nn.silu(z)
        return o.reshape(b_, L, DN_INNER) @ w_out

    m_lat = output(outs_lat[0] + outs_lat[1], zl)
    m_ctx = output(outs_ctx[0] + outs_ctx[1], zc) if ctx_out else None
    return m_ctx, m_lat


def axial_rope(rows, head_dim):
    f32 = jnp.float32
    quarter = head_dim // 4
    inv_freq = ROPE_BASE ** (-jnp.arange(quarter, dtype=f32) / quarter)
    r = jnp.broadcast_to(jnp.arange(rows, dtype=f32)[:, None], (rows, GRID_W)).reshape(-1)
    col = jnp.broadcast_to(jnp.arange(GRID_W, dtype=f32)[None, :], (rows, GRID_W)).reshape(-1)
    ang = jnp.concatenate([r[:, None] * inv_freq, col[:, None] * inv_freq], axis=-1)
    return jnp.cos(ang), jnp.sin(ang)


def apply_rope(x, cos, sin):
    x1, x2 = jnp.split(x.astype(jnp.float32), 2, axis=-1)
    c, s = cos[:, None, None, :], sin[:, None, None, :]
    return jnp.concatenate([x1 * c - x2 * s, x1 * s + x2 * c], axis=-1).astype(x.dtype)


def diff_softmax_attend(q, k, v, lam):
    s = jnp.einsum('bqhcd,bkhcd->bhcqk', q, k).astype(jnp.float32)
    p = jax.nn.softmax(s, axis=-1)
    w = p[:, :, 0] - lam * p[:, :, 1]
    return jnp.einsum('bhqk,bkhe->bqhe', w.astype(v.dtype), v)


def diff_attn_mixer(a_ctx, a_lat, w_qkv, q_norm_g, k_norm_g, lam_vecs, subln_g, w_out,
                    lambda_init, cos, sin, ctx_out):
    d = DA_HEAD_DIM
    b_, L, _ = a_lat.shape
    n_ctx = a_ctx.shape[1]

    def heads_qk(t, n, gain):
        return rms_norm(t.reshape(b_, n, DA_HEADS, 2, d), gain)

    p_lat = a_lat @ w_qkv
    q_l = apply_rope(heads_qk(p_lat[..., :DA_INNER], L, q_norm_g) * d ** -0.5, cos, sin)
    k_l = apply_rope(heads_qk(p_lat[..., DA_INNER:2 * DA_INNER], L, k_norm_g), cos, sin)
    v_l = p_lat[..., 2 * DA_INNER:].reshape(b_, L, DA_HEADS, 2 * d)
    p_ctx = a_ctx @ (w_qkv if ctx_out else w_qkv[:, DA_INNER:])
    k_c = heads_qk(p_ctx[..., -2 * DA_INNER:-DA_INNER], n_ctx, k_norm_g)
    v_c = p_ctx[..., -DA_INNER:].reshape(b_, n_ctx, DA_HEADS, 2 * d)
    lv = lam_vecs.astype(jnp.float32)
    lam = jnp.exp(jnp.sum(lv[0] * lv[1])) - jnp.exp(jnp.sum(lv[2] * lv[3])) + lambda_init

    k_all = jnp.concatenate([k_c, k_l], axis=1)
    v_all = jnp.concatenate([v_c, v_l], axis=1)
    nb = L // DA_QBLOCK
    q_blocks = jnp.moveaxis(q_l.reshape(b_, nb, DA_QBLOCK, DA_HEADS, 2, d), 1, 0)
    o_l = lax.map(lambda qb: diff_softmax_attend(qb, k_all, v_all, lam), q_blocks)
    o_l = jnp.moveaxis(o_l, 0, 1).reshape(b_, L, DA_HEADS, 2 * d)

    def output(o, n):
        o = rms_norm(o, subln_g) * (1.0 - lambda_init)
        return o.reshape(b_, n, DA_INNER) @ w_out

    m_lat = output(o_l, L)
    m_ctx = None
    if ctx_out:
        q_c = heads_qk(p_ctx[..., :DA_INNER], n_ctx, q_norm_g) * d ** -0.5
        m_ctx = output(diff_softmax_attend(q_c, k_c, v_c, lam), n_ctx)
    return m_ctx, m_lat


def peer_ffn(h, w_q, sub_keys, u_tab, v_tab):
    t_, dm = h.shape
    half = PEER_DKEY // 2

    def block(hb):
        tb = hb.shape[0]
        q = (hb @ w_q).reshape(tb, PEER_HEADS, 2, half)
        s = jnp.einsum('thpd,hpnd->thpn', q, sub_keys).astype(jnp.float32)
        s_top, i_top = lax.top_k(s, PEER_TOPK)
        cand_s = (s_top[:, :, 0, :, None] + s_top[:, :, 1, None, :]).reshape(tb, PEER_HEADS, -1)
        cand_i = (i_top[:, :, 0, :, None] * PEER_NKEYS + i_top[:, :, 1, None, :]).reshape(tb, PEER_HEADS, -1)
        best_s, best_pos = lax.top_k(cand_s, PEER_TOPK)
        expert = jnp.take_along_axis(cand_i, best_pos, axis=-1)
        gate = jax.nn.softmax(best_s, axis=-1)
        u_e = jnp.take(u_tab, expert, axis=0)
        v_e = jnp.take(v_tab, expert, axis=0)
        act = jax.nn.gelu(jnp.einsum('thkd,td->thk', u_e, hb), approximate=False)
        return jnp.einsum('thk,thkd->td', (gate * act).astype(hb.dtype), v_e)

    out = lax.map(block, h.reshape(t_ // PEER_TOKBLOCK, PEER_TOKBLOCK, dm))
    return out.reshape(t_, dm)


def setup_inputs(seed: int = 0) -> dict:
    key = jax.random.key(seed)
    ks = jax.random.split(key, 24)
    f32 = jnp.float32

    def nrm(k, shape, scale):
        return jax.random.normal(k, shape, f32) * scale

    def gain(k, shape):
        return 1.0 + 0.02 * jax.random.normal(k, shape, f32)

    dt = jnp.exp(jax.random.uniform(ks[11], (N_A_LAYERS, 2, DN_HEADS), f32,
                                    minval=math.log(1e-3), maxval=math.log(1e-1)))
    return {
        'x': nrm(ks[0], (BATCH, SEQ, D_MODEL), 1.0),
        'c': nrm(ks[1], (BATCH, D_MODEL), 1.0),
        'ctx': nrm(ks[2], (BATCH, CTX_LEN, D_MODEL), 1.0),
        'c_ctx': nrm(ks[3], (D_MODEL,), 1.0),
        'ada_w': nrm(ks[4], (DEPTH, D_MODEL, 6 * D_MODEL), ADA_SCALE * D_MODEL ** -0.5),
        'ada_b': nrm(ks[5], (DEPTH, 6 * D_MODEL), 0.02),
        'norm1_g': gain(ks[6], (DEPTH, D_MODEL)),
        'norm2_g': gain(ks[7], (DEPTH, D_MODEL)),
        'dn_w_in': nrm(ks[8], (N_A_LAYERS, D_MODEL, DN_PROJ), D_MODEL ** -0.5),
        'dn_conv': nrm(ks[9], (N_A_LAYERS, DN_CONV, 3 * DN_INNER), DN_CONV ** -0.5),
        'dn_a_log': jnp.log(jax.random.uniform(ks[10], (N_A_LAYERS, 2, DN_HEADS), f32,
                                               minval=1.0, maxval=16.0)),
        'dn_dt_bias': dt + jnp.log(-jnp.expm1(-dt)),
        'dn_norm_g': gain(ks[12], (N_A_LAYERS, DN_HEAD_DIM)),
        'dn_w_out': nrm(ks[13], (N_A_LAYERS, DN_INNER, D_MODEL), DN_INNER ** -0.5),
        'da_w_qkv': nrm(ks[14], (N_B_LAYERS, D_MODEL, 3 * DA_INNER), D_MODEL ** -0.5),
        'da_q_norm_g': gain(ks[15], (N_B_LAYERS, DA_HEAD_DIM)),
        'da_k_norm_g': gain(ks[16], (N_B_LAYERS, DA_HEAD_DIM)),
        'da_lambda': nrm(ks[17], (N_B_LAYERS, 4, DA_HEAD_DIM), 0.1),
        'da_subln_g': gain(ks[18], (N_B_LAYERS, 2 * DA_HEAD_DIM)),
        'da_w_out': nrm(ks[19], (N_B_LAYERS, DA_INNER, D_MODEL), DA_INNER ** -0.5),
        'peer_wq': nrm(ks[20], (DEPTH, D_MODEL, PEER_HEADS * PEER_DKEY), D_MODEL ** -0.5),
        'peer_keys': nrm(ks[21], (DEPTH, PEER_HEADS, 2, PEER_NKEYS, PEER_DKEY // 2),
                         (PEER_DKEY // 2) ** -0.5),
        'peer_u': nrm(ks[22], (DEPTH, PEER_NEXPERTS, D_MODEL), D_MODEL ** -0.5),
        'peer_v': nrm(ks[23], (DEPTH, PEER_NEXPERTS, D_MODEL), PEER_V_SCALE),
    }


def reference(x, c, ctx, c_ctx, ada_w, ada_b, norm1_g, norm2_g,
              dn_w_in, dn_conv, dn_a_log, dn_dt_bias, dn_norm_g, dn_w_out,
              da_w_qkv, da_q_norm_g, da_k_norm_g, da_lambda, da_subln_g, da_w_out,
              peer_wq, peer_keys, peer_u, peer_v):
    b_, L, dm = x.shape
    n_ctx = ctx.shape[1]
    rows = L // GRID_W
    cos, sin = axial_rope(rows, DA_HEAD_DIM)
    h_lat, h_ctx = x, ctx
    for i in range(DEPTH):
        last = i == DEPTH - 1
        sh1, sc1, gt1, sh2, sc2, gt2 = ada_modulation(c, ada_w[i], ada_b[i])
        csh1, csc1, cgt1, csh2, csc2, cgt2 = ada_modulation(c_ctx, ada_w[i], ada_b[i])
        a_lat = modulate(rms_norm(h_lat, norm1_g[i]), sh1, sc1)
        a_ctx = modulate(rms_norm(h_ctx, norm1_g[i]), csh1, csc1)
        j = i // N_MIXERS
        if i % N_MIXERS == 0:
            m_ctx, m_lat = deltanet_mixer(a_ctx, a_lat, dn_w_in[j], dn_conv[j], dn_a_log[j],
                                          dn_dt_bias[j], dn_norm_g[j], dn_w_out[j], not last)
        else:
            lambda_init = 0.8 - 0.6 * math.exp(-0.3 * i)
            m_ctx, m_lat = diff_attn_mixer(a_ctx, a_lat, da_w_qkv[j], da_q_norm_g[j], da_k_norm_g[j],
                                           da_lambda[j], da_subln_g[j], da_w_out[j],
                                           lambda_init, cos, sin, not last)
        h_lat = h_lat + gt1 * m_lat
        f_lat = modulate(rms_norm(h_lat, norm2_g[i]), sh2, sc2)
        if last:
            y = peer_ffn(f_lat.reshape(-1, dm), peer_wq[i], peer_keys[i], peer_u[i], peer_v[i])
            h_lat = h_lat + gt2 * y.reshape(h_lat.shape)
        else:
            h_ctx = h_ctx + cgt1 * m_ctx
            f_ctx = modulate(rms_norm(h_ctx, norm2_g[i]), csh2, csc2)
            tok = jnp.concatenate([f_ctx, f_lat], axis=1)
            y = peer_ffn(tok.reshape(-1, dm), peer_wq[i], peer_keys[i], peer_u[i], peer_v[i])
            y = y.reshape(tok.shape)
            h_ctx = h_ctx + cgt2 * y[:, :n_ctx]
            h_lat = h_lat + gt2 * y[:, n_ctx:]
    return h_lat
```

```python
import functools
import math

import jax
import jax.numpy as jnp
from jax import lax
from jax.experimental import pallas as pl
from jax.experimental.pallas import tpu as pltpu

F32 = jnp.float32
BF16 = jnp.bfloat16

EPS = 1e-6
LANES = 128
ROW_TILE = 256
GRID_W = 64
ROPE_BASE = 10000.0

DN_HEADS = 8
DN_DIM = 128
DN_CONV = 5
DN_CHUNK = 64

DA_HEADS = 8
DA_DIM = 64

PEER_HEADS = 8
PEER_NKEYS = 128
PEER_TOPK = 16
PEER_TOK_LANES = 128
PEER_TOK_TILE = 256
PEER_ROWS_PER_STEP = 8
PEER_PITCH = 136

VMEM_LIMIT = 56 * 1024 * 1024


def _dot(a, b):
    return jnp.dot(a, b, preferred_element_type=F32)


def _dot_nt(a, b):
    return lax.dot_general(a, b, (((1,), (1,)), ((), ())), preferred_element_type=F32)


def _dot_tn(a, b):
    return lax.dot_general(a, b, (((0,), (0,)), ((), ())), preferred_element_type=F32)


def _silu(x):
    return x * (1.0 / (1.0 + jnp.exp(-x)))


def _norm_modulate(x, g, shift, scale):
    y = x * lax.rsqrt(jnp.mean(x * x, axis=-1, keepdims=True) + EPS) * g
    return y * (1.0 + scale) + shift


def _params(*sem):
    return pltpu.CompilerParams(dimension_semantics=sem, vmem_limit_bytes=VMEM_LIMIT)


def _ada_kernel(cond_ref, w_ref, b_ref, o_ref):
    c = cond_ref[...]
    o_ref[...] = jnp.dot(_silu(c), w_ref[...], preferred_element_type=F32,
                         precision=lax.Precision.HIGHEST) + b_ref[...]


def ada_modulation_all(cond, ada_w, ada_b):
    depth, d, n = ada_w.shape
    r = cond.shape[0]
    out = pl.pallas_call(
        _ada_kernel,
        grid=(depth, n // d),
        in_specs=[pl.BlockSpec((r, d), lambda l, j: (0, 0)),
                  pl.BlockSpec((None, d, d), lambda l, j: (l, 0, j)),
                  pl.BlockSpec((None, 1, d), lambda l, j: (l, 0, j))],
        out_specs=pl.BlockSpec((None, r, d), lambda l, j: (l, 0, j)),
        out_shape=jax.ShapeDtypeStruct((depth, r, n), F32),
        compiler_params=_params("parallel", "parallel"),
        name="ada_modulation",
    )(cond, ada_w, ada_b.reshape(depth, 1, n))
    return out.reshape(depth, r, n // d, d)


def _mod_index(n_batch):
    return lambda b, j: (jnp.where(j == 0, n_batch, b), 0, 0)


def _ln_linear_kernel(x_ref, g_ref, mod_ref, w_ref, o_ref, *, shift_row, scale_row):
    y = _norm_modulate(x_ref[...], g_ref[...], mod_ref[shift_row:shift_row + 1, :],
                       mod_ref[scale_row:scale_row + 1, :])
    o_ref[...] = _dot(y.astype(BF16), w_ref[...]).astype(o_ref.dtype)


def ln_linear(tok, g, mod, w, *, shift_row, scale_row, out_dtype=F32):
    b, s, d = tok.shape
    n = w.shape[1]
    return pl.pallas_call(
        functools.partial(_ln_linear_kernel, shift_row=shift_row, scale_row=scale_row),
        grid=(b, s // ROW_TILE),
        in_specs=[pl.BlockSpec((None, ROW_TILE, d), lambda i, j: (i, j, 0)),
                  pl.BlockSpec((1, d), lambda i, j: (0, 0)),
                  pl.BlockSpec((None, 6, d), _mod_index(b)),
                  pl.BlockSpec((d, n), lambda i, j: (0, 0))],
        out_specs=pl.BlockSpec((None, ROW_TILE, n), lambda i, j: (i, j, 0)),
        out_shape=jax.ShapeDtypeStruct((b, s, n), out_dtype),
        compiler_params=_params("parallel", "parallel"),
        name="ln_linear",
    )(tok, g.reshape(1, d), mod, w)


def _linear_residual_kernel(x_ref, w_ref, res_ref, mod_ref, o_ref, *, gate_row):
    m = _dot(x_ref[...], w_ref[...])
    o_ref[...] = res_ref[...] + mod_ref[gate_row:gate_row + 1, :] * m


def linear_residual(x, w, res, mod, *, gate_row):
    b, s, k = x.shape
    d = w.shape[1]
    return pl.pallas_call(
        functools.partial(_linear_residual_kernel, gate_row=gate_row),
        grid=(b, s // ROW_TILE),
        in_specs=[pl.BlockSpec((None, ROW_TILE, k), lambda i, j: (i, j, 0)),
                  pl.BlockSpec((k, d), lambda i, j: (0, 0)),
                  pl.BlockSpec((None, ROW_TILE, d), lambda i, j: (i, j, 0)),
                  pl.BlockSpec((None, 6, d), _mod_index(b))],
        out_specs=pl.BlockSpec((None, ROW_TILE, d), lambda i, j: (i, j, 0)),
        out_shape=jax.ShapeDtypeStruct((b, s, d), F32),
        compiler_params=_params("parallel", "parallel"),
        name="linear_residual",
    )(x, w, res, mod)


def _dn_conv_kernel(p_ref, w_ref, o_ref, *, n_ctx):
    c = pl.program_id(1)
    x = p_ref[...]
    s = x.shape[0]
    t = lax.broadcasted_iota(jnp.int32, x.shape, 0)
    acc = jnp.zeros_like(x)
    for tap in range(DN_CONV):
        d = tap - DN_CONV // 2
        xs = x if d == 0 else pltpu.roll(x, (-d) % s, 0)
        src = t + d
        ok = (src >= 0) & (src < s) & ((src >= n_ctx) == (t >= n_ctx))
        acc = acc + jnp.where(ok, xs, 0.0) * w_ref[tap:tap + 1, :]
    y = _silu(acc)
    inv = lax.rsqrt(jnp.sum(y * y, axis=-1, keepdims=True) + EPS)
    is_q = c < DN_HEADS
    is_qk = c < 2 * DN_HEADS
    factor = jnp.where(is_qk, inv, 1.0) * jnp.where(is_q, DN_DIM ** -0.5, 1.0)
    o_ref[...] = y * factor


def dn_conv(p, conv_w, n_ctx):
    b, s, _ = p.shape
    nblk = 3 * DN_HEADS
    return pl.pallas_call(
        functools.partial(_dn_conv_kernel, n_ctx=n_ctx),
        grid=(b, nblk),
        in_specs=[pl.BlockSpec((None, s, LANES), lambda i, c: (i, 0, c)),
                  pl.BlockSpec((DN_CONV, LANES), lambda i, c: (0, c))],
        out_specs=pl.BlockSpec((None, None, s, LANES), lambda i, c: (i, c, 0, 0)),
        out_shape=jax.ShapeDtypeStruct((b, nblk, s, LANES), F32),
        compiler_params=_params("parallel", "parallel"),
        name="dn_conv",
    )(p, conv_w)


def _dn_gate_kernel(p_ref, alog_ref, dt_ref, o_ref):
    x = p_ref[...]
    lane = lax.broadcasted_iota(jnp.int32, x.shape, 1)
    beta = 1.0 / (1.0 + jnp.exp(-x))
    z = x + dt_ref[...]
    softplus = jnp.maximum(z, 0.0) + jnp.log1p(jnp.exp(-jnp.abs(z)))
    decay = -jnp.exp(alog_ref[...]) * softplus
    o_ref[...] = jnp.where(lane < 2 * DN_HEADS, beta, decay)


def dn_gates(p, a_log, dt_bias, col_block):
    b, s, _ = p.shape
    pad = jnp.zeros((2 * DN_HEADS,), F32)
    rest = jnp.zeros((LANES - 4 * DN_HEADS,), F32)
    alog = jnp.concatenate([pad, a_log.reshape(-1).astype(F32), rest]).reshape(1, LANES)
    dt = jnp.concatenate([pad, dt_bias.reshape(-1).astype(F32), rest]).reshape(1, LANES)
    return pl.pallas_call(
        _dn_gate_kernel,
        grid=(b, s // ROW_TILE),
        in_specs=[pl.BlockSpec((None, ROW_TILE, LANES), lambda i, j: (i, j, col_block)),
                  pl.BlockSpec((1, LANES), lambda i, j: (0, 0)),
                  pl.BlockSpec((1, LANES), lambda i, j: (0, 0))],
        out_specs=pl.BlockSpec((None, ROW_TILE, LANES), lambda i, j: (i, j, 0)),
        out_shape=jax.ShapeDtypeStruct((b, s, LANES), F32),
        compiler_params=_params("parallel", "parallel"),
        name="dn_gates",
    )(p, alog, dt)


def _dn_chunk_kernel(q_ref, k_ref, v_ref, gt_ref, o_ref, s_ref, *, reverse):
    c = DN_CHUNK

    @pl.when(pl.program_id(1) == 0)
    def _():
        s_ref[...] = jnp.zeros_like(s_ref)

    row = lax.broadcasted_iota(jnp.int32, (c, c), 0)
    col = lax.broadcasted_iota(jnp.int32, (c, c), 1)
    incl = (row <= col) if reverse else (row >= col)
    strict = (row < col) if reverse else (row > col)
    eye = jnp.where(row == col, 1.0, 0.0).astype(F32)
    last = 0 if reverse else c - 1
    direction = 1 if reverse else 0

    gt = gt_ref[...]
    g_all = jnp.dot(jnp.where(incl, 1.0, 0.0).astype(F32), gt, preferred_element_type=F32,
                    precision=lax.Precision.HIGHEST)
    g_all_t = g_all.T

    for h in range(DN_HEADS):
        cb = direction * DN_HEADS + h
        cg = 2 * DN_HEADS + cb
        beta = gt[:, cb:cb + 1]
        g_col = g_all[:, cg:cg + 1]
        g_row = g_all_t[cg:cg + 1, :]
        g_last = g_all[last:last + 1, cg:cg + 1]
        q = q_ref[h]
        k = k_ref[h]
        v = v_ref[h]
        gam = jnp.exp(jnp.where(incl, g_col - g_row, -jnp.inf))
        kb = k * beta
        kb16 = kb.astype(BF16)
        k16 = k.astype(BF16)
        a_mat = jnp.where(strict, _dot_nt(kb16, k16) * gam, 0.0)
        e_g = jnp.exp(g_col)
        rhs = jnp.concatenate([v * beta, kb * e_g], axis=1)
        pw = -a_mat
        inv = eye + pw
        for _ in range(int(math.log2(c)) - 1):
            pw16 = pw.astype(BF16)
            pw = _dot(pw16, pw16)
            inv = inv + _dot(inv.astype(BF16), pw.astype(BF16))
        sol = _dot(inv.astype(BF16), rhs.astype(BF16))
        u = sol[:, :DN_DIM]
        w = sol[:, DN_DIM:]
        qk = _dot_nt(q.astype(BF16), k16) * gam
        q_dec = q * e_g
        k_dec = k * jnp.exp(g_last - g_col)
        s = s_ref[h]
        s16 = s.astype(BF16)
        delta = u - _dot(w.astype(BF16), s16)
        d16 = delta.astype(BF16)
        o_ref[h] = _dot(q_dec.astype(BF16), s16) + _dot(qk.astype(BF16), d16)
        s_ref[h] = s * jnp.exp(g_last) + _dot_tn(k_dec.astype(BF16), d16)


def dn_scan(qkv, gt, n_ctx, *, reverse):
    b, _, s, _ = qkv.shape
    n_chunks = s // DN_CHUNK
    ctx_chunks = n_ctx // DN_CHUNK

    def chunk_of(n):
        if not reverse:
            return n
        return jnp.where(n < ctx_chunks, ctx_chunks - 1 - n, n_chunks - 1 + ctx_chunks - n)

    def head_spec(group):
        return pl.BlockSpec((None, DN_HEADS, DN_CHUNK, DN_DIM), lambda i, n: (i, group, chunk_of(n), 0))

    return pl.pallas_call(
        functools.partial(_dn_chunk_kernel, reverse=reverse),
        grid=(b, n_chunks),
        in_specs=[head_spec(0), head_spec(1), head_spec(2),
                  pl.BlockSpec((None, DN_CHUNK, LANES), lambda i, n: (i, chunk_of(n), 0))],
        out_specs=pl.BlockSpec((None, DN_HEADS, DN_CHUNK, DN_DIM), lambda i, n: (i, 0, chunk_of(n), 0)),
        out_shape=jax.ShapeDtypeStruct((b, DN_HEADS, s, DN_DIM), F32),
        scratch_shapes=[pltpu.VMEM((DN_HEADS, DN_DIM, DN_DIM), F32)],
        compiler_params=_params("parallel", "arbitrary"),
        name="dn_scan_bwd" if reverse else "dn_scan_fwd",
    )(qkv, qkv, qkv, gt)


def _dn_out_kernel(of_ref, ob_ref, z_ref, g_ref, w_ref, res_ref, mod_ref, o_ref, *, gate_row):
    parts = []
    for h in range(DN_HEADS):
        o = of_ref[h] + ob_ref[h]
        y = o * lax.rsqrt(jnp.mean(o * o, axis=-1, keepdims=True) + EPS) * g_ref[...]
        parts.append((y * _silu(z_ref[:, h * DN_DIM:(h + 1) * DN_DIM])).astype(BF16))
    m = _dot(jnp.concatenate(parts, axis=1), w_ref[...])
    o_ref[...] = res_ref[...] + mod_ref[gate_row:gate_row + 1, :] * m


def dn_output(o_f, o_b, p, norm_g, w_out, res, mod, *, gate_row, z_block):
    b, s, d = res.shape
    inner = DN_HEADS * DN_DIM
    return pl.pallas_call(
        functools.partial(_dn_out_kernel, gate_row=gate_row),
        grid=(b, s // ROW_TILE),
        in_specs=[pl.BlockSpec((None, DN_HEADS, ROW_TILE, DN_DIM), lambda i, j: (i, 0, j, 0)),
                  pl.BlockSpec((None, DN_HEADS, ROW_TILE, DN_DIM), lambda i, j: (i, 0, j, 0)),
                  pl.BlockSpec((None, ROW_TILE, inner), lambda i, j: (i, j, z_block)),
                  pl.BlockSpec((1, DN_DIM), lambda i, j: (0, 0)),
                  pl.BlockSpec((inner, d), lambda i, j: (0, 0)),
                  pl.BlockSpec((None, ROW_TILE, d), lambda i, j: (i, j, 0)),
                  pl.BlockSpec((None, 6, d), _mod_index(b))],
        out_specs=pl.BlockSpec((None, ROW_TILE, d), lambda i, j: (i, j, 0)),
        out_shape=jax.ShapeDtypeStruct((b, s, d), F32),
        compiler_params=_params("parallel", "parallel"),
        name="dn_output",
    )(o_f, o_b, p, norm_g.reshape(1, DN_DIM), w_out, res, mod)


def deltanet_layer(tok, mod, n_ctx, norm1_g, w_in, conv_w, a_log, dt_bias, norm_g, w_out):
    d = tok.shape[-1]
    inner = DN_HEADS * DN_DIM
    n_proj = w_in.shape[1]
    n_pad = (-n_proj) % LANES
    w_in16 = jnp.pad(w_in, ((0, 0), (0, n_pad))).astype(BF16)
    p = ln_linear(tok, norm1_g, mod, w_in16, shift_row=0, scale_row=1)
    qkv = dn_conv(p, conv_w, n_ctx)
    gt = dn_gates(p, a_log, dt_bias, (4 * inner) // LANES)
    o_f = dn_scan(qkv, gt, n_ctx, reverse=False)
    o_b = dn_scan(qkv, gt, n_ctx, reverse=True)
    return dn_output(o_f, o_b, p, norm_g, w_out.astype(BF16), tok, mod, gate_row=2,
                     z_block=(3 * inner) // inner)


def _da_prep_kernel(p_ref, cos_ref, sin_ref, qg_ref, kg_ref, o_ref):
    nq = DA_HEADS
    lane = lax.broadcasted_iota(jnp.int32, (ROW_TILE, LANES), 1)
    lo = lane < DA_DIM
    first_half = (lane % DA_DIM) < DA_DIM // 2
    cos = cos_ref[...]
    sin = sin_ref[...]
    for c in range(3 * nq):
        x = p_ref[:, c * LANES:(c + 1) * LANES]
        if c < 2 * nq:
            x2 = x * x
            s_lo = jnp.sum(jnp.where(lo, x2, 0.0), axis=-1, keepdims=True)
            s_hi = jnp.sum(jnp.where(lo, 0.0, x2), axis=-1, keepdims=True)
            ms = jnp.where(lo, s_lo, s_hi) * (1.0 / DA_DIM)
            gain = qg_ref[...] if c < nq else kg_ref[...]
            x = x * lax.rsqrt(ms + EPS) * gain
            if c < nq:
                x = x * DA_DIM ** -0.5
            partner = jnp.where(first_half, pltpu.roll(x, LANES - DA_DIM // 2, 1),
                                pltpu.roll(x, DA_DIM // 2, 1))
            x = x * cos + partner * sin
        o_ref[c] = x.astype(o_ref.dtype)


def da_prep(p, cos, sin, q_gain, k_gain):
    b, s, n = p.shape
    nblk = n // LANES
    qg = jnp.tile(q_gain.astype(F32), 2).reshape(1, LANES)
    kg = jnp.tile(k_gain.astype(F32), 2).reshape(1, LANES)
    return pl.pallas_call(
        _da_prep_kernel,
        grid=(b, s // ROW_TILE),
        in_specs=[pl.BlockSpec((None, ROW_TILE, n), lambda i, j: (i, j, 0)),
                  pl.BlockSpec((ROW_TILE, LANES), lambda i, j: (j, 0)),
                  pl.BlockSpec((ROW_TILE, LANES), lambda i, j: (j, 0)),
                  pl.BlockSpec((1, LANES), lambda i, j: (0, 0)),
                  pl.BlockSpec((1, LANES), lambda i, j: (0, 0))],
        out_specs=pl.BlockSpec((None, nblk, ROW_TILE, LANES), lambda i, j: (i, 0, j, 0)),
        out_shape=jax.ShapeDtypeStruct((b, nblk, s, LANES), BF16),
        compiler_params=_params("parallel", "parallel"),
        name="da_prep",
    )(p, cos, sin, qg, kg)


def _da_attn_kernel(q_ref, k_ref, v_ref, lam_ref, g_ref, o_ref, *, n_ctx, lambda_init):
    lv = lam_ref[...]
    lam = (jnp.exp(jnp.sum(lv[0:1] * lv[1:2], axis=-1, keepdims=True))
           - jnp.exp(jnp.sum(lv[2:3] * lv[3:4], axis=-1, keepdims=True)) + lambda_init)
    q = q_ref[...]
    lane = lax.broadcasted_iota(jnp.int32, q.shape, 1)
    zero = jnp.zeros_like(q)
    q0 = jnp.where(lane < DA_DIM, q, zero)
    q1 = jnp.where(lane < DA_DIM, zero, q)

    def attend(k, v):
        def probs(qc):
            s = _dot_nt(qc, k)
            e = jnp.exp(s - jnp.max(s, axis=-1, keepdims=True))
            return e, 1.0 / jnp.sum(e, axis=-1, keepdims=True)
        e0, r0 = probs(q0)
        e1, r1 = probs(q1)
        w = e0 * r0 - e1 * (lam * r1)
        o = _dot(w.astype(BF16), v)
        y = o * lax.rsqrt(jnp.mean(o * o, axis=-1, keepdims=True) + EPS) * g_ref[...]
        o_ref[...] = (y * (1.0 - lambda_init)).astype(o_ref.dtype)

    is_ctx = pl.program_id(2) == 0

    @pl.when(is_ctx)
    def _():
        attend(k_ref[0:n_ctx, :], v_ref[0:n_ctx, :])

    @pl.when(jnp.logical_not(is_ctx))
    def _():
        attend(k_ref[...], v_ref[...])


def da_attention(qkv, lam_vecs, subln_g, n_ctx, lambda_init):
    b, _, s, _ = qkv.shape
    h = DA_HEADS
    return pl.pallas_call(
        functools.partial(_da_attn_kernel, n_ctx=n_ctx, lambda_init=lambda_init),
        grid=(b, h, s // ROW_TILE),
        in_specs=[pl.BlockSpec((None, None, ROW_TILE, LANES), lambda i, hh, j: (i, hh, j, 0)),
                  pl.BlockSpec((None, None, s, LANES), lambda i, hh, j: (i, h + hh, 0, 0)),
                  pl.BlockSpec((None, None, s, LANES), lambda i, hh, j: (i, 2 * h + hh, 0, 0)),
                  pl.BlockSpec((4, DA_DIM), lambda i, hh, j: (0, 0)),
                  pl.BlockSpec((1, LANES), lambda i, hh, j: (0, 0))],
        out_specs=pl.BlockSpec((None, ROW_TILE, LANES), lambda i, hh, j: (i, j, hh)),
        out_shape=jax.ShapeDtypeStruct((b, s, h * LANES), BF16),
        compiler_params=_params("parallel", "parallel", "arbitrary"),
        name="da_attention",
    )(qkv, qkv, qkv, lam_vecs.astype(F32), subln_g.astype(F32).reshape(1, LANES))


def rope_tables(n_ctx, n_lat):
    quarter = DA_DIM // 4
    inv_freq = ROPE_BASE ** (-jnp.arange(quarter, dtype=F32) / quarter)
    pos = jnp.arange(n_lat)
    r = (pos // GRID_W).astype(F32)
    col = (pos % GRID_W).astype(F32)
    ang = jnp.concatenate([r[:, None] * inv_freq, col[:, None] * inv_freq], axis=-1)
    cos = jnp.concatenate([jnp.ones((n_ctx, DA_DIM // 2), F32), jnp.cos(ang)], axis=0)
    sin = jnp.concatenate([jnp.zeros((n_ctx, DA_DIM // 2), F32), jnp.sin(ang)], axis=0)
    return jnp.tile(cos, (1, 4)), jnp.concatenate([-sin, sin, -sin, sin], axis=1)


def diffattn_layer(tok, mod, n_ctx, norm1_g, w_qkv, q_gain, k_gain, lam_vecs, subln_g, w_out,
                   lambda_init, cos, sin):
    p = ln_linear(tok, norm1_g, mod, w_qkv.astype(BF16), shift_row=0, scale_row=1)
    qkv = da_prep(p, cos, sin, q_gain, k_gain)
    a = da_attention(qkv, lam_vecs, subln_g, n_ctx, lambda_init)
    return linear_residual(a, w_out.astype(BF16), tok, mod, gate_row=2)


def _top_rows(vals, n_top):
    rows = vals.shape[0]
    idx = lax.broadcasted_iota(jnp.int32, vals.shape, 0).astype(F32)
    top_v, top_i = [], []
    for _ in range(n_top):
        m = jnp.max(vals, axis=0, keepdims=True)
        am = jnp.min(jnp.where(vals == m, idx, float(rows)), axis=0, keepdims=True)
        top_v.append(m)
        top_i.append(am)
        vals = jnp.where(idx == am, -jnp.inf, vals)
    return jnp.concatenate(top_v, axis=0), jnp.concatenate(top_i, axis=0)


def _peer_route_kernel(x_ref, g_ref, mod_ref, wq_ref, keys_ref, isel_ref, jsel_ref, gate_ref,
                       q_scr, sv_scr, si_scr, oi_scr, oj_scr, og_scr):
    k = PEER_TOPK
    f = _norm_modulate(x_ref[...], g_ref[...], mod_ref[3:4, :], mod_ref[4:5, :])
    q_scr[...] = _dot(f.astype(BF16), wq_ref[...]).astype(BF16)

    def sub_key(hp, carry):
        qs = q_scr[:, pl.ds(pl.multiple_of(hp * LANES, LANES), LANES)]
        scores = _dot_nt(keys_ref[hp], qs)
        v, i = _top_rows(scores, k)
        sv_scr[hp] = v
        si_scr[hp] = i
        return carry

    lax.fori_loop(0, 2 * PEER_HEADS, sub_key, 0)

    def head(h, carry):
        s0, s1 = sv_scr[2 * h], sv_scr[2 * h + 1]
        i0, i1 = si_scr[2 * h], si_scr[2 * h + 1]
        cand = jnp.concatenate([s0[r:r + 1, :] + s1 for r in range(k)], axis=0)
        best, pos = _top_rows(cand, k)
        k1 = jnp.floor(pos * (1.0 / k))
        k2 = pos - k1 * k
        sub = lax.broadcasted_iota(jnp.int32, (k, x_ref.shape[0]), 0).astype(F32)
        sel_i, sel_j = [], []
        for r in range(k):
            sel_i.append(jnp.sum(jnp.where(sub == k1[r:r + 1, :], i0, 0.0), axis=0, keepdims=True))
            sel_j.append(jnp.sum(jnp.where(sub == k2[r:r + 1, :], i1, 0.0), axis=0, keepdims=True))
        e = jnp.exp(best - jnp.max(best, axis=0, keepdims=True))
        gate = e / jnp.sum(e, axis=0, keepdims=True)
        rows = pl.ds(pl.multiple_of(h * k, k), k)
        oi_scr[rows, :] = jnp.concatenate(sel_i, axis=0)
        oj_scr[rows, :] = jnp.concatenate(sel_j, axis=0)
        og_scr[rows, :] = gate
        return carry

    lax.fori_loop(0, PEER_HEADS, head, 0)
    isel_ref[...] = oi_scr[...].T.astype(jnp.int32)
    jsel_ref[...] = oj_scr[...].T.astype(jnp.int32)
    gate_ref[...] = og_scr[...].T


def _tile_mod_index(tiles_per_batch, ctx_tiles, n_batch):
    return lambda t, *_: (jnp.where(t % tiles_per_batch < ctx_tiles, n_batch, t // tiles_per_batch), 0, 0)


def peer_route(tok2d, norm_g, mod, wq, keys, seq_len, n_ctx):
    t, d = tok2d.shape
    tl = PEER_TOK_LANES
    n_sel = PEER_HEADS * PEER_TOPK
    nq = wq.shape[1]
    n_batch = mod.shape[0] - 1
    out = jax.ShapeDtypeStruct((t, n_sel), jnp.int32)
    return pl.pallas_call(
        _peer_route_kernel,
        grid=(t // tl,),
        in_specs=[pl.BlockSpec((tl, d), lambda i: (i, 0)),
                  pl.BlockSpec((1, d), lambda i: (0, 0)),
                  pl.BlockSpec((None, 6, d), _tile_mod_index(seq_len // tl, n_ctx // tl, n_batch)),
                  pl.BlockSpec((d, nq), lambda i: (0, 0)),
                  pl.BlockSpec(keys.shape, lambda i: (0, 0, 0))],
        out_specs=[pl.BlockSpec((tl, n_sel), lambda i: (i, 0))] * 3,
        out_shape=[out, out, jax.ShapeDtypeStruct((t, n_sel), F32)],
        scratch_shapes=[pltpu.VMEM((tl, nq), BF16),
                        pltpu.VMEM((2 * PEER_HEADS, PEER_TOPK, tl), F32),
                        pltpu.VMEM((2 * PEER_HEADS, PEER_TOPK, tl), F32),
                        pltpu.VMEM((n_sel, tl), F32),
                        pltpu.VMEM((n_sel, tl), F32),
                        pltpu.VMEM((n_sel, tl), F32)],
        compiler_params=_params("parallel"),
        name="peer_route",
    )(tok2d, norm_g.reshape(1, d), mod, wq, keys)


def _gelu(x):
    return 0.5 * x * (1.0 + lax.erf(x * (0.5 ** 0.5)))


def _peer_expert_kernel(x_ref, g_ref, mod_ref, isel_ref, jsel_ref, gate_ref, u_ref, v_ref, o_ref,
                        f_scr, gate_scr):
    c = pl.program_id(1)
    n_tok = x_ref.shape[0]
    n_keys = PEER_NKEYS

    @pl.when(c == 0)
    def _():
        f = _norm_modulate(x_ref[...], g_ref[...], mod_ref[3:4, :], mod_ref[4:5, :])
        f_scr[...] = f.astype(BF16)
        o_ref[...] = jnp.zeros_like(o_ref)
        sub = lax.broadcasted_iota(jnp.int32, (n_keys, n_keys), 0)

        def token(t, carry):
            irow = isel_ref[pl.ds(t, 1), :]
            jrow = jsel_ref[pl.ds(t, 1), :]
            grow = gate_ref[pl.ds(t, 1), :]
            a_t = jnp.where(sub == irow, 1.0, 0.0).astype(BF16)
            b_t = jnp.where(sub == jrow, grow, 0.0).astype(BF16)
            gate_scr[pl.ds(pl.multiple_of(t * PEER_PITCH, 8), n_keys), :] = _dot_nt(a_t, b_t)
            return carry

        lax.fori_loop(0, n_tok, token, 0, unroll=2)

    act = _dot_nt(f_scr[...], u_ref[...])
    parts = []
    for r in range(PEER_ROWS_PER_STEP):
        g = gate_scr[pl.ds(c * PEER_ROWS_PER_STEP + r, n_tok, stride=PEER_PITCH), :]
        parts.append((_gelu(act[:, r * n_keys:(r + 1) * n_keys]) * g).astype(BF16))
    o_ref[...] += _dot(jnp.concatenate(parts, axis=1), v_ref[...])

    @pl.when(c == pl.num_programs(1) - 1)
    def _():
        o_ref[...] = x_ref[...] + mod_ref[5:6, :] * o_ref[...]


def peer_experts(tok2d, norm_g, mod, isel, jsel, gate, u16, v16, seq_len, n_ctx):
    t, d = tok2d.shape
    tt = PEER_TOK_TILE
    n_sel = isel.shape[1]
    n_exp = u16.shape[0]
    ec = PEER_ROWS_PER_STEP * PEER_NKEYS
    n_batch = mod.shape[0] - 1
    return pl.pallas_call(
        _peer_expert_kernel,
        grid=(t // tt, n_exp // ec),
        in_specs=[pl.BlockSpec((tt, d), lambda i, c: (i, 0)),
                  pl.BlockSpec((1, d), lambda i, c: (0, 0)),
                  pl.BlockSpec((None, 6, d), _tile_mod_index(seq_len // tt, n_ctx // tt, n_batch)),
                  pl.BlockSpec((tt, n_sel), lambda i, c: (i, 0)),
                  pl.BlockSpec((tt, n_sel), lambda i, c: (i, 0)),
                  pl.BlockSpec((tt, n_sel), lambda i, c: (i, 0)),
                  pl.BlockSpec((ec, d), lambda i, c: (c, 0)),
                  pl.BlockSpec((ec, d), lambda i, c: (c, 0))],
        out_specs=pl.BlockSpec((tt, d), lambda i, c: (i, 0)),
        out_shape=jax.ShapeDtypeStruct((t, d), F32),
        scratch_shapes=[pltpu.VMEM((tt, d), BF16),
                        pltpu.VMEM((tt * PEER_PITCH, PEER_NKEYS), F32)],
        compiler_params=_params("parallel", "arbitrary"),
        name="peer_experts",
    )(tok2d, norm_g.reshape(1, d), mod, isel, jsel, gate, u16, v16)


def peer_layer(tok, mod, n_ctx, norm2_g, wq, keys, u_tab, v_tab):
    b, s, d = tok.shape
    tok2d = tok.reshape(b * s, d)
    keys16 = keys.reshape(2 * PEER_HEADS, PEER_NKEYS, -1).astype(BF16)
    isel, jsel, gate = peer_route(tok2d, norm2_g, mod, wq.astype(BF16), keys16, s, n_ctx)
    out = peer_experts(tok2d, norm2_g, mod, isel, jsel, gate, u_tab.astype(BF16), v_tab.astype(BF16),
                       s, n_ctx)
    return out.reshape(b, s, d)


def kernel(x, c, ctx, c_ctx, ada_w, ada_b, norm1_g, norm2_g, dn_w_in, dn_conv, dn_a_log, dn_dt_bias, dn_norm_g, dn_w_out, da_w_qkv, da_q_norm_g, da_k_norm_g, da_lambda, da_subln_g, da_w_out, peer_wq, peer_keys, peer_u, peer_v):
    b, n_lat, d = x.shape
    n_ctx = ctx.shape[1]
    depth = ada_w.shape[0]
    assert n_ctx == ROW_TILE and n_lat % ROW_TILE == 0 and n_lat % GRID_W == 0
    assert PEER_TOK_LANES <= ROW_TILE and PEER_TOK_TILE <= ROW_TILE

    cond_rows = -(-(b + 1) // 8) * 8
    cond = jnp.concatenate([c, c_ctx[None, :], jnp.zeros((cond_rows - b - 1, d), F32)], axis=0)
    mod_all = ada_modulation_all(cond, ada_w, ada_b)[:, :b + 1]
    cos, sin = rope_tables(n_ctx, n_lat)

    tok = jnp.concatenate([ctx, x], axis=1)
    for i in range(depth):
        mod = mod_all[i]
        j = i // 2
        if i % 2 == 0:
            tok = deltanet_layer(tok, mod, n_ctx, norm1_g[i], dn_w_in[j], dn_conv[j], dn_a_log[j],
                                 dn_dt_bias[j], dn_norm_g[j], dn_w_out[j])
        else:
            lambda_init = 0.8 - 0.6 * math.exp(-0.3 * i)
            tok = diffattn_layer(tok, mod, n_ctx, norm1_g[i], da_w_qkv[j], da_q_norm_g[j], da_k_norm_g[j],
                                 da_lambda[j], da_subln_g[j], da_w_out[j], lambda_init, cos, sin)
        tok = peer_layer(tok, mod, n_ctx, norm2_g[i], peer_wq[i], peer_keys[i], peer_u[i], peer_v[i])
    return tok[:, n_ctx:, :]
```
